```python
import jax, jax.numpy as jnp
from jax import lax
import numpy as np

D_MODEL = 1024
BATCH = 2
SEQ = 8192
DEPTH = 1

CTX_LEN = 256
GRID_W = 64
HG_HEADS = 8
HG_DK = 128
HG_DV = 128
HG_WK = HG_HEADS * HG_DK
HG_WV = HG_HEADS * HG_DV
HG_CHUNK = 32
RT_HEADS = 8
RT_DK = D_MODEL // RT_HEADS
RT_DV = 2 * D_MODEL // RT_HEADS
RT_WK = RT_HEADS * RT_DK
RT_WV = RT_HEADS * RT_DV
RT_CHUNK = 64
ROPE_BASE = 10000.0
D_FF = ((8 * D_MODEL // 3 + 255) // 256) * 256
EPS = 1e-6
GN_EPS = 1e-5
_SPLIT_SIZES = (HG_WK, HG_WK, HG_WK, HG_WV, HG_WV, RT_WK, RT_WK, RT_WV, RT_WV, D_MODEL, D_MODEL)
D_IN = sum(_SPLIT_SIZES)
_SPLIT_POINTS = tuple(int(v) for v in np.cumsum(_SPLIT_SIZES)[:-1])

kernel_name = "hybrid_hgrn2_retention_dit_block"


def _rmsnorm(x, w):
    xf = x.astype(jnp.float32)
    y = xf * lax.rsqrt(jnp.mean(xf * xf, axis=-1, keepdims=True) + EPS)
    return (y * w.astype(jnp.float32)).astype(x.dtype)


def _modulate(h, shift, scale):
    return h * (1.0 + scale) + shift


def _heads(t, n_heads):
    B, L, _ = t.shape
    return t.reshape(B, L, n_heads, -1).transpose(0, 2, 1, 3)


def _merge_heads(t):
    B, n, L, d = t.shape
    return t.transpose(0, 2, 1, 3).reshape(B, L, n * d)


def _chunk(t, C):
    B, H, L, d = t.shape
    return jnp.moveaxis(t.reshape(B, H, L // C, C, d), 2, 0)


def _unchunk(t):
    n, B, H, C, d = t.shape
    return jnp.moveaxis(t, 0, 2).reshape(B, H, n * C, d)


def _flip(t):
    return jnp.flip(t, axis=2)


def _axial_rope(L):
    rows = L // GRID_W
    row = jnp.repeat(jnp.arange(rows, dtype=jnp.float32), GRID_W)
    col = jnp.tile(jnp.arange(GRID_W, dtype=jnp.float32), rows)
    n_f = RT_DK // 4
    freqs = ROPE_BASE ** (-jnp.arange(n_f, dtype=jnp.float32) / n_f)
    ang = jnp.concatenate([row[:, None] * freqs, col[:, None] * freqs], axis=-1)
    return jnp.cos(ang), jnp.sin(ang)


def _apply_rope(t, cos, sin):
    half = t.shape[-1] // 2
    t1, t2 = t[..., :half], t[..., half:]
    return jnp.concatenate([t1 * cos - t2 * sin, t1 * sin + t2 * cos], axis=-1)


def _hgrn2_scan(q, k, v, lf, s0):
    C = HG_CHUNK
    causal = jnp.tril(jnp.ones((C, C), dtype=bool))[:, :, None]

    def step(s, blk):
        qc, kc, vc, lfc = blk
        b = jnp.cumsum(lfc, axis=2)
        rel = jnp.where(causal, b[:, :, :, None, :] - b[:, :, None, :, :], -jnp.inf)
        att = jnp.sum(qc[:, :, :, None, :] * kc[:, :, None, :, :] * jnp.exp(rel), axis=-1)
        o = (jnp.einsum('bhts,bhse->bhte', att, vc)
             + jnp.einsum('bhtd,bhde->bhte', qc * jnp.exp(b), s))
        b_last = b[:, :, -1:, :]
        s = (jnp.exp(b_last[:, :, 0, :])[..., None] * s
             + jnp.einsum('bhsd,bhse->bhde', kc * jnp.exp(b_last - b), vc))
        return s, o

    _, o = lax.scan(step, s0, (_chunk(q, C), _chunk(k, C), _chunk(v, C), _chunk(lf, C)))
    return _unchunk(o)


def _hgrn2_final_state(k, v, lf):
    b = jnp.cumsum(lf, axis=2)
    return jnp.einsum('bhsd,bhse->bhde', k * jnp.exp(b[:, :, -1:, :] - b), v)


def _retention_scan(q, k, v, log_gamma, s0):
    C = RT_CHUNK
    idx = jnp.arange(C, dtype=jnp.float32)
    diff = idx[:, None] - idx[None, :]
    lg = log_gamma[:, None, None]
    dmat = jnp.where(diff >= 0, jnp.exp(lg * jnp.maximum(diff, 0.0)), 0.0)
    q_decay = jnp.exp(log_gamma[:, None] * (idx + 1.0))[None, :, :, None]
    k_decay = jnp.exp(log_gamma[:, None] * (C - 1.0 - idx))[None, :, :, None]
    s_decay = jnp.exp(log_gamma * C)[None, :, None, None]

    def step(s, blk):
        qc, kc, vc = blk
        att = jnp.einsum('bhtd,bhsd->bhts', qc, kc) * dmat
        o = (jnp.einsum('bhts,bhse->bhte', att, vc)
             + jnp.einsum('bhtd,bhde->bhte', qc * q_decay, s))
        s = s_decay * s + jnp.einsum('bhsd,bhse->bhde', kc * k_decay, vc)
        return s, o

    _, o = lax.scan(step, s0, (_chunk(q, C), _chunk(k, C), _chunk(v, C)))
    return _unchunk(o)


def _retention_final_state(k, v, log_gamma):
    L = k.shape[2]
    w = jnp.exp(log_gamma[:, None] * (L - 1.0 - jnp.arange(L, dtype=jnp.float32)))
    return jnp.einsum('bhsd,hs,bhse->bhde', k, w, v)


def _head_rms(t):
    return t * lax.rsqrt(jnp.mean(t * t, axis=-1, keepdims=True) + EPS)


def _group_norm(t):
    mu = jnp.mean(t, axis=-1, keepdims=True)
    var = jnp.mean(jnp.square(t - mu), axis=-1, keepdims=True)
    return (t - mu) * lax.rsqrt(var + GN_EPS)


def _mixer_inputs(h, w_in, lb_f, lb_b, rope):
    p = jnp.matmul(h, w_in).astype(jnp.float32)
    hq, hf_f, hf_b, hi, hg, rq, rk, rv, rg, ga, gb = jnp.split(p, _SPLIT_POINTS, axis=-1)

    def forget(z, lb):
        k = _heads((1.0 - lb) * jax.nn.sigmoid(-z), HG_HEADS)
        lf = _heads(jnp.log(lb + (1.0 - lb) * jax.nn.sigmoid(z)), HG_HEADS)
        return k, lf

    hk_f, hlf_f = forget(hf_f, lb_f)
    hk_b, hlf_b = forget(hf_b, lb_b)
    rq = _heads(rq, RT_HEADS) * RT_DK ** -0.5
    rk = _heads(rk, RT_HEADS)
    if rope is not None:
        rq = _apply_rope(rq, rope[0], rope[1])
        rk = _apply_rope(rk, rope[0], rope[1])
    return dict(hq=_heads(hq, HG_HEADS) * HG_DK ** -0.5, hk_f=hk_f, hlf_f=hlf_f,
                hk_b=hk_b, hlf_b=hlf_b, hv=_heads(hi, HG_HEADS), hg=hg,
                rq=rq, rk=rk, rv=_heads(rv, RT_HEADS), rg=rg, ga=ga, gb=gb)


def _context_states(p, log_g_f, log_g_b):
    s_hf = _hgrn2_final_state(p["hk_f"], p["hv"], p["hlf_f"])
    s_hb = _hgrn2_final_state(_flip(p["hk_b"]), _flip(p["hv"]), _flip(p["hlf_b"]))
    s_rf = _retention_final_state(p["rk"], p["rv"], log_g_f)
    s_rb = _retention_final_state(_flip(p["rk"]), _flip(p["rv"]), log_g_b)
    return s_hf, s_hb, s_rf, s_rb


def _mix(p, s_hf, s_hb, s_rf, s_rb, log_g_f, log_g_b, hg_norm_w, w_pa, w_pb, w_out):
    hg = (_hgrn2_scan(p["hq"], p["hk_f"], p["hv"], p["hlf_f"], s_hf)
          + _flip(_hgrn2_scan(_flip(p["hq"]), _flip(p["hk_b"]), _flip(p["hv"]),
                              _flip(p["hlf_b"]), s_hb)))
    rt = (_retention_scan(p["rq"], p["rk"], p["rv"], log_g_f, s_rf)
          + _flip(_retention_scan(_flip(p["rq"]), _flip(p["rk"]), _flip(p["rv"]), log_g_b, s_rb)))
    ya = _merge_heads(_head_rms(hg)) * hg_norm_w * jax.nn.silu(p["hg"])
    yb = _merge_heads(_group_norm(rt)) * jax.nn.silu(p["rg"])
    merged = (jax.nn.sigmoid(p["ga"]) * jnp.matmul(ya, w_pa)
              + jax.nn.sigmoid(p["gb"]) * jnp.matmul(yb, w_pb))
    return jnp.matmul(merged, w_out)


def _swiglu(h, w_gate, w_up, w_down):
    return jnp.matmul(jax.nn.silu(jnp.matmul(h, w_gate)) * jnp.matmul(h, w_up), w_down)


def setup_inputs(seed: int = 0) -> dict:
    key = jax.random.key(seed)
    ks = jax.random.split(key, 24)
    f32 = jnp.float32

    def nrm(k, shape, scale):
        return jax.random.normal(k, shape, f32) * scale

    gamma = 1.0 - 2.0 ** (-5.0 - jnp.arange(RT_HEADS, dtype=f32))
    theta0 = jnp.log(gamma) - jnp.log1p(-gamma)
    return {
        "x": nrm(ks[0], (BATCH, SEQ, D_MODEL), 1.0),
        "c": nrm(ks[1], (BATCH, D_MODEL), 1.0),
        "ctx": nrm(ks[2], (BATCH, CTX_LEN, D_MODEL), 1.0),
        "c_ctx": nrm(ks[3], (D_MODEL,), 1.0),
        "w_mod": nrm(ks[4], (DEPTH, D_MODEL, 6 * D_MODEL), 0.5 * D_MODEL ** -0.5),
        "b_mod": nrm(ks[5], (DEPTH, 6 * D_MODEL), 0.02),
        "norm1_w": 1.0 + nrm(ks[6], (DEPTH, D_MODEL), 0.02),
        "norm2_w": 1.0 + nrm(ks[7], (DEPTH, D_MODEL), 0.02),
        "w_in": nrm(ks[8], (DEPTH, D_MODEL, D_IN), D_MODEL ** -0.5),
        "hg_lb_fwd": nrm(ks[9], (DEPTH + 1, HG_WK), 0.1),
        "hg_lb_bwd": nrm(ks[10], (DEPTH + 1, HG_WK), 0.1),
        "hg_norm_w": 1.0 + nrm(ks[11], (DEPTH, HG_WV), 0.02),
        "rt_theta_fwd": theta0 + nrm(ks[12], (DEPTH, RT_HEADS), 0.1),
        "rt_theta_bwd": theta0 + nrm(ks[13], (DEPTH, RT_HEADS), 0.1),
        "w_proj_hgrn": nrm(ks[14], (DEPTH, HG_WV, D_MODEL), HG_WV ** -0.5),
        "w_proj_ret": nrm(ks[15], (DEPTH, RT_WV, D_MODEL), RT_WV ** -0.5),
        "w_out": nrm(ks[16], (DEPTH, D_MODEL, D_MODEL), D_MODEL ** -0.5),
        "w_ffn_gate": nrm(ks[17], (DEPTH, D_MODEL, D_FF), D_MODEL ** -0.5),
        "w_ffn_up": nrm(ks[18], (DEPTH, D_MODEL, D_FF), D_MODEL ** -0.5),
        "w_ffn_down": nrm(ks[19], (DEPTH, D_FF, D_MODEL), D_FF ** -0.5),
        "final_norm_w": 1.0 + nrm(ks[20], (D_MODEL,), 0.02),
    }


def reference(x, c, ctx, c_ctx, w_mod, b_mod, norm1_w, norm2_w, w_in, hg_lb_fwd, hg_lb_bwd,
              hg_norm_w, rt_theta_fwd, rt_theta_bwd, w_proj_hgrn, w_proj_ret, w_out,
              w_ffn_gate, w_ffn_up, w_ffn_down, final_norm_w):
    B, L, _ = x.shape
    f32 = jnp.float32
    rope = _axial_rope(L)
    lb_f_all = jnp.cumsum(jax.nn.softmax(hg_lb_fwd.astype(f32), axis=0), axis=0)
    lb_b_all = jnp.cumsum(jax.nn.softmax(hg_lb_bwd.astype(f32), axis=0), axis=0)
    hg_zero = jnp.zeros((B, HG_HEADS, HG_DK, HG_DV), f32)
    rt_zero = jnp.zeros((B, RT_HEADS, RT_DK, RT_DV), f32)

    for l in range(DEPTH):
        last = l == DEPTH - 1
        mod_x = (jnp.matmul(jax.nn.silu(c), w_mod[l]) + b_mod[l])[:, None, :]
        mod_c = jnp.matmul(jax.nn.silu(c_ctx), w_mod[l]) + b_mod[l]
        sh1, sc1, g1, sh2, sc2, g2 = jnp.split(mod_x, 6, axis=-1)
        sh1c, sc1c, g1c, sh2c, sc2c, g2c = jnp.split(mod_c, 6, axis=-1)
        log_g_f = jax.nn.log_sigmoid(rt_theta_fwd[l].astype(f32))
        log_g_b = jax.nn.log_sigmoid(rt_theta_bwd[l].astype(f32))
        branch_w = (hg_norm_w[l], w_proj_hgrn[l], w_proj_ret[l], w_out[l])

        hc = _modulate(_rmsnorm(ctx, norm1_w[l]), sh1c, sc1c)
        pc = _mixer_inputs(hc, w_in[l], lb_f_all[l], lb_b_all[l], None)
        s_hf, s_hb, s_rf, s_rb = _context_states(pc, log_g_f, log_g_b)
        if not last:
            ctx_mix = _mix(pc, hg_zero, hg_zero, rt_zero, rt_zero, log_g_f, log_g_b, *branch_w)
            ctx = ctx + g1c * ctx_mix.astype(ctx.dtype)
            hc2 = _modulate(_rmsnorm(ctx, norm2_w[l]), sh2c, sc2c)
            ctx = ctx + g2c * _swiglu(hc2, w_ffn_gate[l], w_ffn_up[l], w_ffn_down[l])

        hx = _modulate(_rmsnorm(x, norm1_w[l]), sh1, sc1)
        px = _mixer_inputs(hx, w_in[l], lb_f_all[l], lb_b_all[l], rope)
        x_mix = _mix(px, s_hf, s_hb, s_rf, s_rb, log_g_f, log_g_b, *branch_w)
        x = x + g1 * x_mix.astype(x.dtype)
        hx2 = _modulate(_rmsnorm(x, norm2_w[l]), sh2, sc2)
        x = x + g2 * _swiglu(hx2, w_ffn_gate[l], w_ffn_up[l], w_ffn_down[l])

    return _rmsnorm(x, final_norm_w)
```

```python
import functools

import jax
import jax.numpy as jnp
from jax import lax
from jax.experimental import pallas as pl
from jax.experimental.pallas import tpu as pltpu

F32 = jnp.float32
BF16 = jnp.bfloat16

D_MODEL = 1024
N_HEADS = 8
HG_D = 128
RT_DK = 128
RT_DV = 256
D_FF = 2816
EPS = 1e-6
GN_EPS = 1e-5
GRID_W = 64
ROPE_BASE = 10000.0
D_IN = 13 * D_MODEL
COL_HQ, COL_ZF, COL_ZB, COL_HV, COL_HG = 0, 1, 2, 3, 4
COL_RQ, COL_RK, COL_RV0, COL_RV1, COL_RG0, COL_RG1, COL_GA, COL_GB = 5, 6, 7, 8, 9, 10, 11, 12
N_COL = 13

HG_CHUNK = 128
RT_CHUNK = 256
HG_FACTOR_MIN_LOGDECAY = -60.0
FF_SPLIT = 2
VMEM_LIMIT = 56 * 1024 * 1024


def _sigmoid(x):
    return 1.0 / (1.0 + jnp.exp(-x))


def _dot(a, b):
    return jnp.dot(a, b, preferred_element_type=F32)


def _dot_nt(a, b):
    return lax.dot_general(a, b, (((1,), (1,)), ((), ())), preferred_element_type=F32)


def _dot_tn(a, b):
    return lax.dot_general(a, b, (((0,), (0,)), ((), ())), preferred_element_type=F32)


def _params(sem):
    return pltpu.CompilerParams(dimension_semantics=sem, vmem_limit_bytes=VMEM_LIMIT)


def _mod_kernel(c_ref, w_ref, b_ref, o_ref):
    c = c_ref[...]
    s = c * _sigmoid(c)
    o_ref[...] = jnp.dot(s, w_ref[...], preferred_element_type=F32,
                         precision=lax.Precision.HIGHEST) + b_ref[...]


def _mod_call(c_rows, w_mod, b_mod):
    rows = c_rows.shape[0]
    n = w_mod.shape[1]
    tn = D_MODEL
    return pl.pallas_call(
        _mod_kernel,
        grid=(n // tn,),
        in_specs=[pl.BlockSpec((rows, D_MODEL), lambda j: (0, 0)),
                  pl.BlockSpec((D_MODEL, tn), lambda j: (0, j)),
                  pl.BlockSpec((1, tn), lambda j: (0, j))],
        out_specs=pl.BlockSpec((rows, tn), lambda j: (0, j)),
        out_shape=jax.ShapeDtypeStruct((rows, n), F32),
        compiler_params=_params(("arbitrary",)),
        name="mod",
    )(c_rows, w_mod, b_mod)


def _proj_kernel(x_ref, sh_ref, sc_ref, nw_ref, w_ref, cos_ref, sin_ref, o_ref, hx_ref, *, use_rope):
    j = pl.program_id(2)

    @pl.when(j == 0)
    def _():
        x = x_ref[0]
        r = lax.rsqrt(jnp.mean(x * x, axis=-1, keepdims=True) + EPS)
        h = (x * r) * nw_ref[...]
        hx_ref[...] = (h * (1.0 + sc_ref[0]) + sh_ref[0]).astype(BF16)

    acc = _dot(hx_ref[...], w_ref[...])

    def rope(t):
        cosf = cos_ref[...]
        sinf = sin_ref[...]
        for h in range(N_HEADS):
            hs = slice(h * RT_DK, (h + 1) * RT_DK)
            th = t[:, hs]
            o_ref[0, :, hs] = (th * cosf + pltpu.roll(th, RT_DK // 2, axis=1) * sinf).astype(BF16)

    @pl.when(j == COL_HQ)
    def _():
        o_ref[0] = (acc * HG_D ** -0.5).astype(BF16)

    @pl.when(j == COL_RQ)
    def _():
        if use_rope:
            rope(acc * RT_DK ** -0.5)
        else:
            o_ref[0] = (acc * RT_DK ** -0.5).astype(BF16)

    @pl.when(j == COL_RK)
    def _():
        if use_rope:
            rope(acc)
        else:
            o_ref[0] = acc.astype(BF16)

    @pl.when((j != COL_HQ) & (j != COL_RQ) & (j != COL_RK))
    def _():
        o_ref[0] = acc.astype(BF16)


def _proj_call(x, sh, sc, norm_w, w_bf16, cosf, sinf, *, tm, use_rope):
    B, L, _ = x.shape
    kern = functools.partial(_proj_kernel, use_rope=use_rope)
    return pl.pallas_call(
        kern,
        grid=(B, L // tm, N_COL),
        in_specs=[pl.BlockSpec((1, tm, D_MODEL), lambda b, i, j: (b, i, 0)),
                  pl.BlockSpec((1, 1, D_MODEL), lambda b, i, j: (b, 0, 0)),
                  pl.BlockSpec((1, 1, D_MODEL), lambda b, i, j: (b, 0, 0)),
                  pl.BlockSpec((1, D_MODEL), lambda b, i, j: (0, 0)),
                  pl.BlockSpec((D_MODEL, D_MODEL), lambda b, i, j: (0, j)),
                  pl.BlockSpec((tm, RT_DK), lambda b, i, j: (i, 0)),
                  pl.BlockSpec((tm, RT_DK), lambda b, i, j: (i, 0))],
        out_specs=pl.BlockSpec((1, tm, D_MODEL), lambda b, i, j: (b, i, j)),
        out_shape=jax.ShapeDtypeStruct((B, L, D_IN), BF16),
        scratch_shapes=[pltpu.VMEM((tm, D_MODEL), BF16)],
        compiler_params=_params(("arbitrary", "arbitrary", "arbitrary")),
        name="proj",
    )(x, sh, sc, norm_w, w_bf16, cosf, sinf)


def _hgrn_kernel(qf_ref, qb_ref, zf_ref, zb_ref, vf_ref, vb_ref, lbf_ref, lbb_ref, s0f_ref, s0b_ref,
                 of_ref, ob_ref, sf_ref, sb_ref, bscr_ref, qscr_ref, oscr_ref):
    C = HG_CHUNK
    n = pl.program_id(1)

    @pl.when(n == 0)
    def _():
        sf_ref[...] = s0f_ref[...]
        sb_ref[...] = s0b_ref[...]

    row = lax.broadcasted_iota(jnp.int32, (C, C), 0)
    col = lax.broadcasted_iota(jnp.int32, (C, C), 1)
    srow = lax.broadcasted_iota(jnp.int32, (C, HG_D), 0)
    dirs = (
        (qf_ref, zf_ref, vf_ref, lbf_ref, of_ref, sf_ref, row >= col, C - 1),
        (qb_ref, zb_ref, vb_ref, lbb_ref, ob_ref, sb_ref, row <= col, 0),
    )

    chains = []
    dec_min = None
    for q_ref, z_ref, v_ref, lb_ref, o_ref, s_ref, keep, last in dirs:
        tri = keep.astype(BF16)
        for h in range(N_HEADS):
            hs = slice(h * HG_D, (h + 1) * HG_D)
            z = z_ref[0, :, hs].astype(F32)
            k = (1.0 - lb_ref[:, hs]) / (1.0 + jnp.exp(z))
            lf = jnp.log(1.0 - k)
            hi = lf.astype(BF16)
            lo = (lf - hi.astype(F32)).astype(BF16)
            b = _dot(tri, hi) + _dot(tri, lo)
            b_last = b[last:last + 1, :]
            m = jnp.min(b_last)
            dec_min = m if dec_min is None else jnp.minimum(dec_min, m)
            q = q_ref[0, :, hs].astype(F32)
            qt = (q * jnp.exp(b)).astype(BF16)
            kt = (k * jnp.exp(b_last - b)).astype(BF16)
            chains.append((hs, h, q, k, b, b_last, qt, kt, v_ref, o_ref, s_ref, keep, last))

    def finish(chain, o_intra):
        hs, h, q, k, b, b_last, qt, kt, v_ref, o_ref, s_ref, keep, last = chain
        st = s_ref[0, h]
        o = o_intra + _dot_nt(qt, st.astype(BF16))
        o_ref[0, :, hs] = o.astype(BF16)
        s_ref[0, h] = st * jnp.exp(b_last) + _dot_tn(v_ref[0, :, hs], kt)

    factor_ok = dec_min >= HG_FACTOR_MIN_LOGDECAY

    @pl.when(factor_ok)
    def _():
        for chain in chains:
            hs, h, q, k, b, b_last, qt, kt, v_ref, o_ref, s_ref, keep, last = chain
            kinv = (k * jnp.exp(-b)).astype(BF16)
            att = jnp.where(keep, _dot_nt(qt, kinv), 0.0).astype(BF16)
            finish(chain, _dot(att, v_ref[0, :, hs]))

    @pl.when(jnp.logical_not(factor_ok))
    def _():
        for chain in chains:
            hs, h, q, k, b, b_last, qt, kt, v_ref, o_ref, s_ref, keep, last = chain
            bscr_ref[...] = b
            qscr_ref[...] = q
            v = v_ref[0, :, hs].astype(F32)
            fwd = last != 0

            def body(t, carry, k=k, b=b, v=v, fwd=fwd):
                valid = (srow <= t) if fwd else (srow >= t)
                e = jnp.where(valid, bscr_ref[pl.ds(t, 1), :] - b, 0.0)
                w = jnp.where(valid, jnp.exp(e), 0.0) * k * qscr_ref[pl.ds(t, 1), :]
                a = jnp.sum(w, axis=1, keepdims=True)
                oscr_ref[pl.ds(t, 1), :] = jnp.sum(a * v, axis=0, keepdims=True)
                return carry

            lax.fori_loop(0, C, body, 0)
            finish(chain, oscr_ref[...])


def _hgrn_call(P, lb_f, lb_b, s0_f, s0_b):
    B, L, _ = P.shape
    C = HG_CHUNK
    N = L // C

    def blk(col, rev):
        if rev:
            return pl.BlockSpec((1, C, D_MODEL), lambda b, n: (b, N - 1 - n, col))
        return pl.BlockSpec((1, C, D_MODEL), lambda b, n: (b, n, col))

    st_spec = pl.BlockSpec((1, N_HEADS, HG_D, HG_D), lambda b, n: (b, 0, 0, 0))
    lb_spec = pl.BlockSpec((1, D_MODEL), lambda b, n: (0, 0))
    o_shape = jax.ShapeDtypeStruct((B, L, D_MODEL), BF16)
    s_shape = jax.ShapeDtypeStruct((B, N_HEADS, HG_D, HG_D), F32)
    return pl.pallas_call(
        _hgrn_kernel,
        grid=(B, N),
        in_specs=[blk(COL_HQ, False), blk(COL_HQ, True), blk(COL_ZF, False), blk(COL_ZB, True),
                  blk(COL_HV, False), blk(COL_HV, True), lb_spec, lb_spec, st_spec, st_spec],
        out_specs=[pl.BlockSpec((1, C, D_MODEL), lambda b, n: (b, n, 0)),
                   pl.BlockSpec((1, C, D_MODEL), lambda b, n: (b, N - 1 - n, 0)),
                   st_spec, st_spec],
        out_shape=[o_shape, o_shape, s_shape, s_shape],
        scratch_shapes=[pltpu.VMEM((C, HG_D), F32), pltpu.VMEM((C, HG_D), F32),
                        pltpu.VMEM((C, HG_D), F32)],
        compiler_params=_params(("arbitrary", "arbitrary")),
        name="hgrn",
    )(P, P, P, P, P, P, lb_f, lb_b, s0_f, s0_b)


def _ret_kernel(lgf_ref, lgb_ref, qf_ref, qb_ref, kf_ref, kb_ref, v0f_ref, v1f_ref, v0b_ref, v1b_ref,
                s0f_ref, s0b_ref, of_ref, ob_ref, sf_ref, sb_ref):
    C = RT_CHUNK
    n = pl.program_id(1)

    @pl.when(n == 0)
    def _():
        sf_ref[...] = s0f_ref[...]
        sb_ref[...] = s0b_ref[...]

    t_i = lax.broadcasted_iota(jnp.int32, (C, C), 0)
    s_i = lax.broadcasted_iota(jnp.int32, (C, C), 1)
    tcol = lax.broadcasted_iota(jnp.int32, (C, RT_DK), 0).astype(F32)
    ones_dv = jnp.ones((1, RT_DV), F32)
    dirs = (
        (lgf_ref, qf_ref, kf_ref, (v0f_ref, v1f_ref), of_ref, sf_ref, True),
        (lgb_ref, qb_ref, kb_ref, (v0b_ref, v1b_ref), ob_ref, sb_ref, False),
    )
    for lg_ref, q_ref, k_ref, v_refs, o_ref, s_ref, fwd in dirs:
        diff = ((t_i - s_i) if fwd else (s_i - t_i)).astype(F32)
        for h in range(N_HEADS):
            lg = lg_ref[h]
            hs = slice(h * RT_DK, (h + 1) * RT_DK)
            vs = slice((h % 4) * RT_DV, (h % 4 + 1) * RT_DV)
            q = q_ref[0, :, hs]
            k = k_ref[0, :, hs]
            v = v_refs[h // 4][0, :, vs]
            dmat = jnp.where(diff >= 0.0, jnp.exp(lg * jnp.maximum(diff, 0.0)), 0.0)
            att = (_dot_nt(q, k) * dmat).astype(BF16)
            if fwd:
                q_dec = jnp.exp(lg * (tcol + 1.0))
                k_dec = jnp.exp(lg * (C - 1.0 - tcol))
            else:
                q_dec = jnp.exp(lg * (C - tcol))
                k_dec = jnp.exp(lg * tcol)
            qd = (q.astype(F32) * q_dec).astype(BF16)
            kd = (k.astype(F32) * k_dec).astype(BF16)
            st = s_ref[0, h]
            o = _dot(att, v) + _dot(qd, st.astype(BF16))
            o_ref[0, :, h * RT_DV:(h + 1) * RT_DV] = o.astype(BF16)
            s_ref[0, h] = st * jnp.exp((lg * C) * ones_dv) + _dot_tn(kd, v)


def _ret_call(P, lg_f, lg_b, s0_f, s0_b):
    B, L, _ = P.shape
    C = min(RT_CHUNK, L)
    assert C == RT_CHUNK
    N = L // C

    def blk(col, rev):
        if rev:
            return pl.BlockSpec((1, C, D_MODEL), lambda b, n: (b, N - 1 - n, col))
        return pl.BlockSpec((1, C, D_MODEL), lambda b, n: (b, n, col))

    smem = pl.BlockSpec(memory_space=pltpu.SMEM)
    st_spec = pl.BlockSpec((1, N_HEADS, RT_DK, RT_DV), lambda b, n: (b, 0, 0, 0))
    o_shape = jax.ShapeDtypeStruct((B, L, N_HEADS * RT_DV), BF16)
    s_shape = jax.ShapeDtypeStruct((B, N_HEADS, RT_DK, RT_DV), F32)
    return pl.pallas_call(
        _ret_kernel,
        grid=(B, N),
        in_specs=[smem, smem,
                  blk(COL_RQ, False), blk(COL_RQ, True), blk(COL_RK, False), blk(COL_RK, True),
                  blk(COL_RV0, False), blk(COL_RV1, False), blk(COL_RV0, True), blk(COL_RV1, True),
                  st_spec, st_spec],
        out_specs=[pl.BlockSpec((1, C, N_HEADS * RT_DV), lambda b, n: (b, n, 0)),
                   pl.BlockSpec((1, C, N_HEADS * RT_DV), lambda b, n: (b, N - 1 - n, 0)),
                   st_spec, st_spec],
        out_shape=[o_shape, o_shape, s_shape, s_shape],
        compiler_params=_params(("arbitrary", "arbitrary")),
        name="ret",
    )(lg_f, lg_b, P, P, P, P, P, P, P, P, s0_f, s0_b)


def _post_kernel(hf_ref, hb_ref, rf_ref, rb_ref, hg_ref, rg0_ref, rg1_ref, ga_ref, gb_ref, x_ref,
                 g1_ref, sh2_ref, sc2_ref, hgw_ref, n2w_ref, wpa_ref, wpb_ref, wout_ref,
                 x1_ref, hx2_ref, ya_ref, yb_ref):
    for h in range(N_HEADS):
        hs = slice(h * HG_D, (h + 1) * HG_D)
        t = hf_ref[0, :, hs].astype(F32) + hb_ref[0, :, hs].astype(F32)
        r = lax.rsqrt(jnp.mean(t * t, axis=-1, keepdims=True) + EPS)
        g = hg_ref[0, :, hs].astype(F32)
        ya_ref[:, hs] = (t * r * hgw_ref[:, hs] * (g * _sigmoid(g))).astype(BF16)
    rg_refs = (rg0_ref, rg1_ref)
    for h in range(N_HEADS):
        vs = slice(h * RT_DV, (h + 1) * RT_DV)
        gs = slice((h % 4) * RT_DV, (h % 4 + 1) * RT_DV)
        t = rf_ref[0, :, vs].astype(F32) + rb_ref[0, :, vs].astype(F32)
        mu = jnp.mean(t, axis=-1, keepdims=True)
        c = t - mu
        var = jnp.mean(c * c, axis=-1, keepdims=True)
        g = rg_refs[h // 4][0, :, gs].astype(F32)
        yb_ref[:, vs] = (c * lax.rsqrt(var + GN_EPS) * (g * _sigmoid(g))).astype(BF16)
    pa = _dot(ya_ref[...], wpa_ref[...])
    pb = _dot(yb_ref[...], wpb_ref[...])
    merged = _sigmoid(ga_ref[0].astype(F32)) * pa + _sigmoid(gb_ref[0].astype(F32)) * pb
    mix = _dot(merged.astype(BF16), wout_ref[...])
    x1 = x_ref[0] + g1_ref[0] * mix
    x1_ref[0] = x1
    r = lax.rsqrt(jnp.mean(x1 * x1, axis=-1, keepdims=True) + EPS)
    hx2_ref[0] = ((x1 * r) * n2w_ref[...] * (1.0 + sc2_ref[0]) + sh2_ref[0]).astype(BF16)


def _resident(shape):
    return pl.BlockSpec(shape, lambda b, i: (0,) * len(shape), pipeline_mode=pl.Buffered(1))


def _post_call(hg_f, hg_b, rt_f, rt_b, P, x, g1, sh2, sc2, hg_norm_w, norm2_w, wpa, wpb, wout, *, tm):
    B, L, _ = x.shape

    def tok(width, col=0):
        return pl.BlockSpec((1, tm, width), lambda b, i: (b, i, col))

    vec = pl.BlockSpec((1, 1, D_MODEL), lambda b, i: (b, 0, 0))
    row = pl.BlockSpec((1, D_MODEL), lambda b, i: (0, 0))
    return pl.pallas_call(
        _post_kernel,
        grid=(B, L // tm),
        in_specs=[tok(D_MODEL), tok(D_MODEL), tok(2 * D_MODEL), tok(2 * D_MODEL),
                  tok(D_MODEL, COL_HG), tok(D_MODEL, COL_RG0), tok(D_MODEL, COL_RG1),
                  tok(D_MODEL, COL_GA), tok(D_MODEL, COL_GB), tok(D_MODEL),
                  vec, vec, vec, row, row,
                  _resident((D_MODEL, D_MODEL)), _resident((2 * D_MODEL, D_MODEL)),
                  _resident((D_MODEL, D_MODEL))],
        out_specs=[tok(D_MODEL), tok(D_MODEL)],
        out_shape=[jax.ShapeDtypeStruct((B, L, D_MODEL), F32),
                   jax.ShapeDtypeStruct((B, L, D_MODEL), BF16)],
        scratch_shapes=[pltpu.VMEM((tm, D_MODEL), BF16), pltpu.VMEM((tm, 2 * D_MODEL), BF16)],
        compiler_params=_params(("arbitrary", "arbitrary")),
        name="post",
    )(hg_f, hg_b, rt_f, rt_b, P, P, P, P, P, x, g1, sh2, sc2, hg_norm_w, norm2_w, wpa, wpb, wout)


def _ffn_kernel(h_ref, x1_ref, g2_ref, fw_ref, wg_ref, wu_ref, wd_ref, o_ref):
    h = h_ref[0]
    width = D_FF // FF_SPLIT
    acc = None
    for c in range(FF_SPLIT):
        cs = slice(c * width, (c + 1) * width)
        g = _dot(h, wg_ref[:, cs])
        u = _dot(h, wu_ref[:, cs])
        a = ((g * _sigmoid(g)) * u).astype(BF16)
        part = _dot(a, wd_ref[cs, :])
        acc = part if acc is None else acc + part
    x2 = x1_ref[0] + g2_ref[0] * acc
    r = lax.rsqrt(jnp.mean(x2 * x2, axis=-1, keepdims=True) + EPS)
    o_ref[0] = (x2 * r) * fw_ref[...]


def _ffn_call(hx2, x1, g2, final_w, wg, wu, wd, *, tm):
    B, L, _ = x1.shape
    tok = pl.BlockSpec((1, tm, D_MODEL), lambda b, i: (b, i, 0))
    return pl.pallas_call(
        _ffn_kernel,
        grid=(B, L // tm),
        in_specs=[tok, tok,
                  pl.BlockSpec((1, 1, D_MODEL), lambda b, i: (b, 0, 0)),
                  pl.BlockSpec((1, D_MODEL), lambda b, i: (0, 0)),
                  _resident((D_MODEL, D_FF)), _resident((D_MODEL, D_FF)), _resident((D_FF, D_MODEL))],
        out_specs=tok,
        out_shape=jax.ShapeDtypeStruct((B, L, D_MODEL), F32),
        compiler_params=_params(("arbitrary", "arbitrary")),
        name="ffn",
    )(hx2, x1, g2, final_w, wg, wu, wd)


def _rope_tables(L):
    rows = L // GRID_W
    row = jnp.repeat(jnp.arange(rows, dtype=F32), GRID_W)
    col = jnp.tile(jnp.arange(GRID_W, dtype=F32), rows)
    n_f = RT_DK // 4
    freqs = ROPE_BASE ** (-jnp.arange(n_f, dtype=F32) / n_f)
    ang = jnp.concatenate([row[:, None] * freqs, col[:, None] * freqs], axis=-1)
    cos, sin = jnp.cos(ang), jnp.sin(ang)
    return jnp.concatenate([cos, cos], axis=-1), jnp.concatenate([-sin, sin], axis=-1)


def kernel(x, c, ctx, c_ctx, w_mod, b_mod, norm1_w, norm2_w, w_in, hg_lb_fwd, hg_lb_bwd, hg_norm_w,
           rt_theta_fwd, rt_theta_bwd, w_proj_hgrn, w_proj_ret, w_out, w_ffn_gate, w_ffn_up,
           w_ffn_down, final_norm_w):
    B, L, D = x.shape
    depth = w_mod.shape[0]
    assert depth == 1 and D == D_MODEL and L % RT_CHUNK == 0 and ctx.shape[1] % RT_CHUNK == 0
    l = 0
    lb_f = jnp.cumsum(jax.nn.softmax(hg_lb_fwd.astype(F32), axis=0), axis=0)[l][None, :]
    lb_b = jnp.cumsum(jax.nn.softmax(hg_lb_bwd.astype(F32), axis=0), axis=0)[l][None, :]
    lg_f = jax.nn.log_sigmoid(rt_theta_fwd[l].astype(F32))
    lg_b = jax.nn.log_sigmoid(rt_theta_bwd[l].astype(F32))

    c_rows = jnp.zeros((8, D), F32).at[:B].set(c).at[B].set(c_ctx)
    mod = _mod_call(c_rows, w_mod[l], b_mod[l][None, :])
    mod_x = mod[:B].reshape(B, 6, 1, D)
    sh1, sc1, g1, sh2, sc2, g2 = (mod_x[:, i] for i in range(6))
    mod_c = jnp.broadcast_to(mod[B].reshape(1, 6, 1, D), (B, 6, 1, D))
    sh1c, sc1c = mod_c[:, 0], mod_c[:, 1]

    w_in_b = w_in[l].astype(BF16)
    n1w = norm1_w[l][None, :]
    cosf, sinf = _rope_tables(L)

    Lc = ctx.shape[1]
    Pc = _proj_call(ctx, sh1c, sc1c, n1w, w_in_b, cosf[:Lc], sinf[:Lc], tm=Lc, use_rope=False)
    hz = jnp.zeros((B, N_HEADS, HG_D, HG_D), F32)
    rz = jnp.zeros((B, N_HEADS, RT_DK, RT_DV), F32)
    _, _, s_hf, s_hb = _hgrn_call(Pc, lb_f, lb_b, hz, hz)
    _, _, s_rf, s_rb = _ret_call(Pc, lg_f, lg_b, rz, rz)

    P = _proj_call(x, sh1, sc1, n1w, w_in_b, cosf, sinf, tm=512, use_rope=True)
    hg_f, hg_b, _, _ = _hgrn_call(P, lb_f, lb_b, s_hf, s_hb)
    rt_f, rt_b, _, _ = _ret_call(P, lg_f, lg_b, s_rf, s_rb)
    x1, hx2 = _post_call(hg_f, hg_b, rt_f, rt_b, P, x, g1, sh2, sc2, hg_norm_w[l][None, :],
                         norm2_w[l][None, :], w_proj_hgrn[l].astype(BF16), w_proj_ret[l].astype(BF16),
                         w_out[l].astype(BF16), tm=256)
    return _ffn_call(hx2, x1, g2, final_norm_w[None, :], w_ffn_gate[l].astype(BF16),
                     w_ffn_up[l].astype(BF16), w_ffn_down[l].astype(BF16), tm=256)
```

```python
import functools

import jax
import jax.numpy as jnp
from jax import lax
from jax.experimental import pallas as pl
from jax.experimental.pallas import tpu as pltpu

F32 = jnp.float32
BF16 = jnp.bfloat16

D_MODEL = 1024
N_HEADS = 8
HG_D = 128
RT_DK = 128
RT_DV = 256
D_FF = 2816
EPS = 1e-6
GN_EPS = 1e-5
GRID_W = 64
ROPE_BASE = 10000.0
D_IN = 13 * D_MODEL
COL_HQ, COL_ZF, COL_ZB, COL_HV, COL_HG = 0, 1, 2, 3, 4
COL_RQ, COL_RK, COL_RV0, COL_RV1, COL_RG0, COL_RG1, COL_GA, COL_GB = 5, 6, 7, 8, 9, 10, 11, 12
N_COL = 13

HG_CHUNK = 128
RT_CHUNK = 256
HG_FACTOR_MAX_SPAN = 60.0
FF_SPLIT = 2
VMEM_LIMIT = 56 * 1024 * 1024


def _sigmoid(x):
    return 1.0 / (1.0 + jnp.exp(-x))


def _dot(a, b):
    return jnp.dot(a, b, preferred_element_type=F32)


def _dot_nt(a, b):
    return lax.dot_general(a, b, (((1,), (1,)), ((), ())), preferred_element_type=F32)


def _dot_tn(a, b):
    return lax.dot_general(a, b, (((0,), (0,)), ((), ())), preferred_element_type=F32)


def _params(sem):
    return pltpu.CompilerParams(dimension_semantics=sem, vmem_limit_bytes=VMEM_LIMIT)


def _mod_kernel(c_ref, w_ref, b_ref, o_ref):
    c = c_ref[...]
    s = c * _sigmoid(c)
    o_ref[...] = jnp.dot(s, w_ref[...], preferred_element_type=F32,
                         precision=lax.Precision.HIGHEST) + b_ref[...]


def _mod_call(c_rows, w_mod, b_mod):
    rows = c_rows.shape[0]
    n = w_mod.shape[1]
    tn = D_MODEL
    return pl.pallas_call(
        _mod_kernel,
        grid=(n // tn,),
        in_specs=[pl.BlockSpec((rows, D_MODEL), lambda j: (0, 0)),
                  pl.BlockSpec((D_MODEL, tn), lambda j: (0, j)),
                  pl.BlockSpec((1, tn), lambda j: (0, j))],
        out_specs=pl.BlockSpec((rows, tn), lambda j: (0, j)),
        out_shape=jax.ShapeDtypeStruct((rows, n), F32),
        compiler_params=_params(("arbitrary",)),
        name="mod",
    )(c_rows, w_mod, b_mod)


def _proj_kernel(x_ref, sh_ref, sc_ref, nw_ref, w_ref, cos_ref, sin_ref, o_ref, hx_ref, *, use_rope):
    j = pl.program_id(2)

    @pl.when(j == 0)
    def _():
        x = x_ref[0]
        r = lax.rsqrt(jnp.mean(x * x, axis=-1, keepdims=True) + EPS)
        h = (x * r) * nw_ref[...]
        hx_ref[...] = (h * (1.0 + sc_ref[0]) + sh_ref[0]).astype(BF16)

    def project(scale, rotary):
        for p in range(N_HEADS // 2):
            t = _dot(hx_ref[...], w_ref[:, p * 2 * RT_DK:(p + 1) * 2 * RT_DK]) * scale
            for h in range(2):
                th = t[:, h * RT_DK:(h + 1) * RT_DK]
                if rotary:
                    th = th * cos_ref[...] + pltpu.roll(th, RT_DK // 2, axis=1) * sin_ref[...]
                hs = slice((2 * p + h) * RT_DK, (2 * p + h + 1) * RT_DK)
                o_ref[0, :, hs] = th.astype(BF16)

    @pl.when(j == COL_HQ)
    def _():
        project(HG_D ** -0.5, False)

    @pl.when(j == COL_RQ)
    def _():
        project(RT_DK ** -0.5, use_rope)

    @pl.when(j == COL_RK)
    def _():
        project(1.0, use_rope)

    @pl.when((j != COL_HQ) & (j != COL_RQ) & (j != COL_RK))
    def _():
        o_ref[0] = _dot(hx_ref[...], w_ref[...]).astype(BF16)


def _proj_call(x, sh, sc, norm_w, w_bf16, cosf, sinf, *, tm, use_rope):
    B, L, _ = x.shape
    kern = functools.partial(_proj_kernel, use_rope=use_rope)
    return pl.pallas_call(
        kern,
        grid=(B, L // tm, N_COL),
        in_specs=[pl.BlockSpec((1, tm, D_MODEL), lambda b, i, j: (b, i, 0)),
                  pl.BlockSpec((1, 1, D_MODEL), lambda b, i, j: (b, 0, 0)),
                  pl.BlockSpec((1, 1, D_MODEL), lambda b, i, j: (b, 0, 0)),
                  pl.BlockSpec((1, D_MODEL), lambda b, i, j: (0, 0)),
                  pl.BlockSpec((D_MODEL, D_MODEL), lambda b, i, j: (0, j)),
                  pl.BlockSpec((tm, RT_DK), lambda b, i, j: (i, 0)),
                  pl.BlockSpec((tm, RT_DK), lambda b, i, j: (i, 0))],
        out_specs=pl.BlockSpec((1, tm, D_MODEL), lambda b, i, j: (b, i, j)),
        out_shape=jax.ShapeDtypeStruct((B, L, D_IN), BF16),
        scratch_shapes=[pltpu.VMEM((tm, D_MODEL), BF16)],
        compiler_params=_params(("arbitrary", "arbitrary", "arbitrary")),
        name="proj",
    )(x, sh, sc, norm_w, w_bf16, cosf, sinf)


def _hgrn_kernel(qf_ref, qb_ref, zf_ref, zb_ref, vf_ref, vb_ref, lbf_ref, lbb_ref, s0f_ref, s0b_ref,
                 of_ref, ob_ref, sf_ref, sb_ref, bscr_ref, qscr_ref, oscr_ref):
    C = HG_CHUNK
    n = pl.program_id(1)

    @pl.when(n == 0)
    def _():
        sf_ref[...] = s0f_ref[...]
        sb_ref[...] = s0b_ref[...]

    row = lax.broadcasted_iota(jnp.int32, (C, C), 0)
    col = lax.broadcasted_iota(jnp.int32, (C, C), 1)
    srow = lax.broadcasted_iota(jnp.int32, (C, HG_D), 0)
    mid = C // 2
    dirs = (
        (qf_ref, zf_ref, vf_ref, lbf_ref, of_ref, sf_ref, row >= col, 0, C - 1),
        (qb_ref, zb_ref, vb_ref, lbb_ref, ob_ref, sb_ref, row <= col, C - 1, 0),
    )

    chains = []
    span = None
    for q_ref, z_ref, v_ref, lb_ref, o_ref, s_ref, keep, first, last in dirs:
        tri = keep.astype(BF16)
        for h in range(N_HEADS):
            hs = slice(h * HG_D, (h + 1) * HG_D)
            z = z_ref[0, :, hs].astype(F32)
            k = (1.0 - lb_ref[:, hs]) / (1.0 + jnp.exp(z))
            lf = jnp.log(1.0 - k)
            hi = lf.astype(BF16)
            lo = (lf - hi.astype(F32)).astype(BF16)
            b = _dot(tri, hi) + _dot(tri, lo)
            b_mid = b[mid:mid + 1, :]
            b_last = b[last:last + 1, :]
            s = jnp.maximum(b[first:first + 1, :] - b_mid, b_mid - b_last)
            span = s if span is None else jnp.maximum(span, s)
            chains.append((hs, h, k, b, b_mid, b_last, q_ref, v_ref, o_ref, s_ref, keep, last))

    def finish(chain, o_intra, qt, kt):
        hs, h, k, b, b_mid, b_last, q_ref, v_ref, o_ref, s_ref, keep, last = chain
        st = s_ref[0, h]
        o = o_intra + _dot_nt(qt, st.astype(BF16))
        o_ref[0, :, hs] = o.astype(BF16)
        s_ref[0, h] = st * jnp.exp(b_last) + _dot_tn(v_ref[0, :, hs], kt)

    factor_ok = jnp.max(span) <= HG_FACTOR_MAX_SPAN

    @pl.when(factor_ok)
    def _():
        for chain in chains:
            hs, h, k, b, b_mid, b_last, q_ref, v_ref, o_ref, s_ref, keep, last = chain
            qm = q_ref[0, :, hs].astype(F32) * jnp.exp(b - b_mid)
            km = k * jnp.exp(b_mid - b)
            att = jnp.where(keep, _dot_nt(qm.astype(BF16), km.astype(BF16)), 0.0).astype(BF16)
            qt = (qm * jnp.exp(b_mid)).astype(BF16)
            kt = (km * jnp.exp(b_last - b_mid)).astype(BF16)
            finish(chain, _dot(att, v_ref[0, :, hs]), qt, kt)

    @pl.when(jnp.logical_not(factor_ok))
    def _():
        for chain in chains:
            hs, h, k, b, b_mid, b_last, q_ref, v_ref, o_ref, s_ref, keep, last = chain
            q = q_ref[0, :, hs].astype(F32)
            qt = (q * jnp.exp(b)).astype(BF16)
            kt = (k * jnp.exp(b_last - b)).astype(BF16)
            bscr_ref[...] = b
            qscr_ref[...] = q
            v = v_ref[0, :, hs].astype(F32)
            fwd = last != 0

            def body(t, carry, k=k, b=b, v=v, fwd=fwd):
                valid = (srow <= t) if fwd else (srow >= t)
                e = jnp.where(valid, bscr_ref[pl.ds(t, 1), :] - b, 0.0)
                w = jnp.where(valid, jnp.exp(e), 0.0) * k * qscr_ref[pl.ds(t, 1), :]
                a = jnp.sum(w, axis=1, keepdims=True)
                oscr_ref[pl.ds(t, 1), :] = jnp.sum(a * v, axis=0, keepdims=True)
                return carry

            lax.fori_loop(0, C, body, 0)
            finish(chain, oscr_ref[...], qt, kt)


def _hgrn_call(P, lb_f, lb_b, s0_f, s0_b):
    B, L, _ = P.shape
    C = HG_CHUNK
    N = L // C

    def blk(col, rev):
        if rev:
            return pl.BlockSpec((1, C, D_MODEL), lambda b, n: (b, N - 1 - n, col))
        return pl.BlockSpec((1, C, D_MODEL), lambda b, n: (b, n, col))

    st_spec = pl.BlockSpec((1, N_HEADS, HG_D, HG_D), lambda b, n: (b, 0, 0, 0))
    lb_spec = pl.BlockSpec((1, D_MODEL), lambda b, n: (0, 0))
    o_shape = jax.ShapeDtypeStruct((B, L, D_MODEL), BF16)
    s_shape = jax.ShapeDtypeStruct((B, N_HEADS, HG_D, HG_D), F32)
    return pl.pallas_call(
        _hgrn_kernel,
        grid=(B, N),
        in_specs=[blk(COL_HQ, False), blk(COL_HQ, True), blk(COL_ZF, False), blk(COL_ZB, True),
                  blk(COL_HV, False), blk(COL_HV, True), lb_spec, lb_spec, st_spec, st_spec],
        out_specs=[pl.BlockSpec((1, C, D_MODEL), lambda b, n: (b, n, 0)),
                   pl.BlockSpec((1, C, D_MODEL), lambda b, n: (b, N - 1 - n, 0)),
                   st_spec, st_spec],
        out_shape=[o_shape, o_shape, s_shape, s_shape],
        scratch_shapes=[pltpu.VMEM((C, HG_D), F32), pltpu.VMEM((C, HG_D), F32),
                        pltpu.VMEM((C, HG_D), F32)],
        compiler_params=_params(("arbitrary", "arbitrary")),
        name="hgrn",
    )(P, P, P, P, P, P, lb_f, lb_b, s0_f, s0_b)


def _ret_kernel(lgf_ref, lgb_ref, qf_ref, qb_ref, kf_ref, kb_ref, v0f_ref, v1f_ref, v0b_ref, v1b_ref,
                s0f_ref, s0b_ref, of_ref, ob_ref, sf_ref, sb_ref):
    C = RT_CHUNK
    n = pl.program_id(1)

    @pl.when(n == 0)
    def _():
        sf_ref[...] = s0f_ref[...]
        sb_ref[...] = s0b_ref[...]

    t_i = lax.broadcasted_iota(jnp.int32, (C, C), 0)
    s_i = lax.broadcasted_iota(jnp.int32, (C, C), 1)
    tcol = lax.broadcasted_iota(jnp.int32, (C, RT_DK), 0).astype(F32)
    ones_dv = jnp.ones((1, RT_DV), F32)
    dirs = (
        (lgf_ref, qf_ref, kf_ref, (v0f_ref, v1f_ref), of_ref, sf_ref, True),
        (lgb_ref, qb_ref, kb_ref, (v0b_ref, v1b_ref), ob_ref, sb_ref, False),
    )
    for lg_ref, q_ref, k_ref, v_refs, o_ref, s_ref, fwd in dirs:
        diff = ((t_i - s_i) if fwd else (s_i - t_i)).astype(F32)
        for h in range(N_HEADS):
            lg = lg_ref[h]
            hs = slice(h * RT_DK, (h + 1) * RT_DK)
            vs = slice((h % 4) * RT_DV, (h % 4 + 1) * RT_DV)
            q = q_ref[0, :, hs]
            k = k_ref[0, :, hs]
            v = v_refs[h // 4][0, :, vs]
            dmat = jnp.where(diff >= 0.0, jnp.exp(lg * jnp.maximum(diff, 0.0)), 0.0)
            att = (_dot_nt(q, k) * dmat).astype(BF16)
            if fwd:
                q_dec = jnp.exp(lg * (tcol + 1.0))
                k_dec = jnp.exp(lg * (C - 1.0 - tcol))
            else:
                q_dec = jnp.exp(lg * (C - tcol))
                k_dec = jnp.exp(lg * tcol)
            qd = (q.astype(F32) * q_dec).astype(BF16)
            kd = (k.astype(F32) * k_dec).astype(BF16)
            st = s_ref[0, h]
            o = _dot(att, v) + _dot(qd, st.astype(BF16))
            o_ref[0, :, h * RT_DV:(h + 1) * RT_DV] = o.astype(BF16)
            s_ref[0, h] = st * jnp.exp((lg * C) * ones_dv) + _dot_tn(kd, v)


def _ret_call(P, lg_f, lg_b, s0_f, s0_b):
    B, L, _ = P.shape
    C = min(RT_CHUNK, L)
    assert C == RT_CHUNK
    N = L // C

    def blk(col, rev):
        if rev:
            return pl.BlockSpec((1, C, D_MODEL), lambda b, n: (b, N - 1 - n, col))
        return pl.BlockSpec((1, C, D_MODEL), lambda b, n: (b, n, col))

    smem = pl.BlockSpec(memory_space=pltpu.SMEM)
    st_spec = pl.BlockSpec((1, N_HEADS, RT_DK, RT_DV), lambda b, n: (b, 0, 0, 0))
    o_shape = jax.ShapeDtypeStruct((B, L, N_HEADS * RT_DV), BF16)
    s_shape = jax.ShapeDtypeStruct((B, N_HEADS, RT_DK, RT_DV), F32)
    return pl.pallas_call(
        _ret_kernel,
        grid=(B, N),
        in_specs=[smem, smem,
                  blk(COL_RQ, False), blk(COL_RQ, True), blk(COL_RK, False), blk(COL_RK, True),
                  blk(COL_RV0, False), blk(COL_RV1, False), blk(COL_RV0, True), blk(COL_RV1, True),
                  st_spec, st_spec],
        out_specs=[pl.BlockSpec((1, C, N_HEADS * RT_DV), lambda b, n: (b, n, 0)),
                   pl.BlockSpec((1, C, N_HEADS * RT_DV), lambda b, n: (b, N - 1 - n, 0)),
                   st_spec, st_spec],
        out_shape=[o_shape, o_shape, s_shape, s_shape],
        compiler_params=_params(("arbitrary", "arbitrary")),
        name="ret",
    )(lg_f, lg_b, P, P, P, P, P, P, P, P, s0_f, s0_b)


def _post_kernel(hf_ref, hb_ref, rf_ref, rb_ref, hg_ref, rg0_ref, rg1_ref, ga_ref, gb_ref, x_ref,
                 g1_ref, sh2_ref, sc2_ref, hgw_ref, n2w_ref, wpa_ref, wpb_ref, wout_ref,
                 x1_ref, hx2_ref, ya_ref, yb_ref):
    for h in range(N_HEADS):
        hs = slice(h * HG_D, (h + 1) * HG_D)
        t = hf_ref[0, :, hs].astype(F32) + hb_ref[0, :, hs].astype(F32)
        r = lax.rsqrt(jnp.mean(t * t, axis=-1, keepdims=True) + EPS)
        g = hg_ref[0, :, hs].astype(F32)
        ya_ref[:, hs] = (t * r * hgw_ref[:, hs] * (g * _sigmoid(g))).astype(BF16)
    rg_refs = (rg0_ref, rg1_ref)
    for h in range(N_HEADS):
        vs = slice(h * RT_DV, (h + 1) * RT_DV)
        gs = slice((h % 4) * RT_DV, (h % 4 + 1) * RT_DV)
        t = rf_ref[0, :, vs].astype(F32) + rb_ref[0, :, vs].astype(F32)
        mu = jnp.mean(t, axis=-1, keepdims=True)
        c = t - mu
        var = jnp.mean(c * c, axis=-1, keepdims=True)
        g = rg_refs[h // 4][0, :, gs].astype(F32)
        yb_ref[:, vs] = (c * lax.rsqrt(var + GN_EPS) * (g * _sigmoid(g))).astype(BF16)
    pa = _dot(ya_ref[...], wpa_ref[...])
    pb = _dot(yb_ref[...], wpb_ref[...])
    merged = _sigmoid(ga_ref[0].astype(F32)) * pa + _sigmoid(gb_ref[0].astype(F32)) * pb
    mix = _dot(merged.astype(BF16), wout_ref[...])
    x1 = x_ref[0] + g1_ref[0] * mix
    x1_ref[0] = x1
    r = lax.rsqrt(jnp.mean(x1 * x1, axis=-1, keepdims=True) + EPS)
    hx2_ref[0] = ((x1 * r) * n2w_ref[...] * (1.0 + sc2_ref[0]) + sh2_ref[0]).astype(BF16)


def _resident(shape):
    return pl.BlockSpec(shape, lambda b, i: (0,) * len(shape), pipeline_mode=pl.Buffered(1))


def _post_call(hg_f, hg_b, rt_f, rt_b, P, x, g1, sh2, sc2, hg_norm_w, norm2_w, wpa, wpb, wout, *, tm):
    B, L, _ = x.shape

    def tok(width, col=0):
        return pl.BlockSpec((1, tm, width), lambda b, i: (b, i, col))

    vec = pl.BlockSpec((1, 1, D_MODEL), lambda b, i: (b, 0, 0))
    row = pl.BlockSpec((1, D_MODEL), lambda b, i: (0, 0))
    return pl.pallas_call(
        _post_kernel,
        grid=(B, L // tm),
        in_specs=[tok(D_MODEL), tok(D_MODEL), tok(2 * D_MODEL), tok(2 * D_MODEL),
                  tok(D_MODEL, COL_HG), tok(D_MODEL, COL_RG0), tok(D_MODEL, COL_RG1),
                  tok(D_MODEL, COL_GA), tok(D_MODEL, COL_GB), tok(D_MODEL),
                  vec, vec, vec, row, row,
                  _resident((D_MODEL, D_MODEL)), _resident((2 * D_MODEL, D_MODEL)),
                  _resident((D_MODEL, D_MODEL))],
        out_specs=[tok(D_MODEL), tok(D_MODEL)],
        out_shape=[jax.ShapeDtypeStruct((B, L, D_MODEL), F32),
                   jax.ShapeDtypeStruct((B, L, D_MODEL), BF16)],
        scratch_shapes=[pltpu.VMEM((tm, D_MODEL), BF16), pltpu.VMEM((tm, 2 * D_MODEL), BF16)],
        compiler_params=_params(("arbitrary", "arbitrary")),
        name="post",
    )(hg_f, hg_b, rt_f, rt_b, P, P, P, P, P, x, g1, sh2, sc2, hg_norm_w, norm2_w, wpa, wpb, wout)


def _ffn_kernel(h_ref, x1_ref, g2_ref, fw_ref, wg_ref, wu_ref, wd_ref, o_ref):
    h = h_ref[0]
    width = D_FF // FF_SPLIT
    acc = None
    for c in range(FF_SPLIT):
        cs = slice(c * width, (c + 1) * width)
        g = _dot(h, wg_ref[:, cs])
        u = _dot(h, wu_ref[:, cs])
        a = ((g * _sigmoid(g)) * u).astype(BF16)
        part = _dot(a, wd_ref[cs, :])
        acc = part if acc is None else acc + part
    x2 = x1_ref[0] + g2_ref[0] * acc
    r = lax.rsqrt(jnp.mean(x2 * x2, axis=-1, keepdims=True) + EPS)
    o_ref[0] = (x2 * r) * fw_ref[...]


def _ffn_call(hx2, x1, g2, final_w, wg, wu, wd, *, tm):
    B, L, _ = x1.shape
    tok = pl.BlockSpec((1, tm, D_MODEL), lambda b, i: (b, i, 0))
    return pl.pallas_call(
        _ffn_kernel,
        grid=(B, L // tm),
        in_specs=[tok, tok,
                  pl.BlockSpec((1, 1, D_MODEL), lambda b, i: (b, 0, 0)),
                  pl.BlockSpec((1, D_MODEL), lambda b, i: (0, 0)),
                  _resident((D_MODEL, D_FF)), _resident((D_MODEL, D_FF)), _resident((D_FF, D_MODEL))],
        out_specs=tok,
        out_shape=jax.ShapeDtypeStruct((B, L, D_MODEL), F32),
        compiler_params=_params(("arbitrary", "arbitrary")),
        name="ffn",
    )(hx2, x1, g2, final_w, wg, wu, wd)


def _rope_tables(L):
    rows = L // GRID_W
    row = jnp.repeat(jnp.arange(rows, dtype=F32), GRID_W)
    col = jnp.tile(jnp.arange(GRID_W, dtype=F32), rows)
    n_f = RT_DK // 4
    freqs = ROPE_BASE ** (-jnp.arange(n_f, dtype=F32) / n_f)
    ang = jnp.concatenate([row[:, None] * freqs, col[:, None] * freqs], axis=-1)
    cos, sin = jnp.cos(ang), jnp.sin(ang)
    return jnp.concatenate([cos, cos], axis=-1), jnp.concatenate([-sin, sin], axis=-1)


def kernel(x, c, ctx, c_ctx, w_mod, b_mod, norm1_w, norm2_w, w_in, hg_lb_fwd, hg_lb_bwd, hg_norm_w,
           rt_theta_fwd, rt_theta_bwd, w_proj_hgrn, w_proj_ret, w_out, w_ffn_gate, w_ffn_up,
           w_ffn_down, final_norm_w):
    B, L, D = x.shape
    depth = w_mod.shape[0]
    assert depth == 1 and D == D_MODEL and L % RT_CHUNK == 0 and ctx.shape[1] % RT_CHUNK == 0
    l = 0
    lb_f = jnp.cumsum(jax.nn.softmax(hg_lb_fwd.astype(F32), axis=0), axis=0)[l][None, :]
    lb_b = jnp.cumsum(jax.nn.softmax(hg_lb_bwd.astype(F32), axis=0), axis=0)[l][None, :]
    lg_f = jax.nn.log_sigmoid(rt_theta_fwd[l].astype(F32))
    lg_b = jax.nn.log_sigmoid(rt_theta_bwd[l].astype(F32))

    c_rows = jnp.zeros((8, D), F32).at[:B].set(c).at[B].set(c_ctx)
    mod = _mod_call(c_rows, w_mod[l], b_mod[l][None, :])
    mod_x = mod[:B].reshape(B, 6, 1, D)
    sh1, sc1, g1, sh2, sc2, g2 = (mod_x[:, i] for i in range(6))
    mod_c = jnp.broadcast_to(mod[B].reshape(1, 6, 1, D), (B, 6, 1, D))
    sh1c, sc1c = mod_c[:, 0], mod_c[:, 1]

    w_in_b = w_in[l].astype(BF16)
    n1w = norm1_w[l][None, :]
    cosf, sinf = _rope_tables(L)

    Lc = ctx.shape[1]
    Pc = _proj_call(ctx, sh1c, sc1c, n1w, w_in_b, cosf[:Lc], sinf[:Lc], tm=Lc, use_rope=False)
    hz = jnp.zeros((B, N_HEADS, HG_D, HG_D), F32)
    rz = jnp.zeros((B, N_HEADS, RT_DK, RT_DV), F32)
    _, _, s_hf, s_hb = _hgrn_call(Pc, lb_f, lb_b, hz, hz)
    _, _, s_rf, s_rb = _ret_call(Pc, lg_f, lg_b, rz, rz)

    P = _proj_call(x, sh1, sc1, n1w, w_in_b, cosf, sinf, tm=1024, use_rope=True)
    hg_f, hg_b, _, _ = _hgrn_call(P, lb_f, lb_b, s_hf, s_hb)
    rt_f, rt_b, _, _ = _ret_call(P, lg_f, lg_b, s_rf, s_rb)
    x1, hx2 = _post_call(hg_f, hg_b, rt_f, rt_b, P, x, g1, sh2, sc2, hg_norm_w[l][None, :],
                         norm2_w[l][None, :], w_proj_hgrn[l].astype(BF16), w_proj_ret[l].astype(BF16),
                         w_out[l].astype(BF16), tm=256)
    return _ffn_call(hx2, x1, g2, final_norm_w[None, :], w_ffn_gate[l].astype(BF16),
                     w_ffn_up[l].astype(BF16), w_ffn_down[l].astype(BF16), tm=256)
```
